```python
import jax, jax.numpy as jnp
from jax import lax
import numpy as np

D_MODEL = 2048
BATCH = 4
SEQ = 4096
DEPTH = 2

SWA_Q_HEADS = 16
SWA_KV_HEADS = 2
SWA_HEAD_DIM = 64
SWA_WINDOW = 128
SWA_BLOCK = 128
MLA_HEADS = 16
MLA_Q_RANK = 512
MLA_KV_RANK = 512
MLA_NOPE_DIM = 128
MLA_ROPE_DIM = 64
MLA_V_DIM = 128
MLA_BLOCK = 128
ROPE_THETA = 10000.0
SGU_GROUPS = 8
SGU_GROUP_DIM = 128
SGU_CHUNK = 128
SGU_WIDTH = SGU_GROUPS * SGU_GROUP_DIM
D_FF = 5632
CONV_WIDTH = 3
N_BRANCHES = 3
EPS = 1e-5
MASK_VALUE = -1e30
DN_ALPHA = (2 * DEPTH) ** 0.25
DN_BETA = (8 * DEPTH) ** -0.25

A_Q = SWA_Q_HEADS * SWA_HEAD_DIM
A_KV = SWA_KV_HEADS * SWA_HEAD_DIM
B_OUT = MLA_HEADS * MLA_V_DIM
N_IN = A_Q + 2 * A_KV + MLA_Q_RANK + MLA_KV_RANK + MLA_ROPE_DIM + 2 * SGU_WIDTH + N_BRANCHES * D_MODEL

kernel_name = "hybrid_swa_mla_sgu_deepnorm"


def _layer_norm(x, g, b):
    xf = x.astype(jnp.float32)
    mu = xf.mean(-1, keepdims=True)
    var = jnp.mean(jnp.square(xf - mu), -1, keepdims=True)
    y = (xf - mu) * lax.rsqrt(var + EPS) * g.astype(jnp.float32) + b.astype(jnp.float32)
    return y.astype(x.dtype)


def _rms_norm(x, g):
    xf = x.astype(jnp.float32)
    y = xf * lax.rsqrt(jnp.mean(jnp.square(xf), -1, keepdims=True) + EPS) * g.astype(jnp.float32)
    return y.astype(x.dtype)


def _rope(x, cos, sin):
    x1, x2 = jnp.split(x, 2, axis=-1)
    return jnp.concatenate([x1 * cos - x2 * sin, x2 * cos + x1 * sin], axis=-1)


def _sliding_window_gqa(q, k, v, sinks):
    B, S = q.shape[:2]
    nb = S // SWA_BLOCK
    G = SWA_Q_HEADS // SWA_KV_HEADS
    qb = q.reshape(B, nb, SWA_BLOCK, SWA_KV_HEADS, G, SWA_HEAD_DIM)
    kb = k.reshape(B, nb, SWA_BLOCK, SWA_KV_HEADS, SWA_HEAD_DIM)
    vb = v.reshape(B, nb, SWA_BLOCK, SWA_KV_HEADS, SWA_HEAD_DIM)

    def with_prev(t):
        prev = jnp.pad(t[:, :-1], ((0, 0), (1, 0), (0, 0), (0, 0), (0, 0)))
        return jnp.concatenate([prev, t], axis=2)

    kw, vw = with_prev(kb), with_prev(vb)
    scores = jnp.einsum('bnqhgd,bnkhd->bnhgqk', qb, kw,
                        preferred_element_type=jnp.float32) * (SWA_HEAD_DIM ** -0.5)
    q_off = jnp.arange(SWA_BLOCK)[:, None] + SWA_BLOCK
    k_off = jnp.arange(2 * SWA_BLOCK)[None, :]
    rel = q_off - k_off
    band = (rel >= 0) & (rel < SWA_WINDOW)
    not_first = (jnp.arange(nb) > 0)[:, None, None]
    valid = band[None] & (not_first | (k_off >= SWA_BLOCK)[None])
    scores = jnp.where(valid[None, :, None, None], scores, MASK_VALUE)
    sink = sinks.astype(jnp.float32).reshape(SWA_KV_HEADS, G)[None, None, :, :, None, None]
    m = jnp.maximum(scores.max(-1, keepdims=True), sink)
    p = jnp.exp(scores - m)
    p = (p / (p.sum(-1, keepdims=True) + jnp.exp(sink - m))).astype(v.dtype)
    out = jnp.einsum('bnhgqk,bnkhd->bnqhgd', p, vw)
    return out.reshape(B, S, A_Q)


def _mla(c_q, c_kv, k_rope, cos, sin, q_norm_g, kv_norm_g, w_uq, w_ukv):
    B, S = c_q.shape[:2]
    q = (_rms_norm(c_q, q_norm_g) @ w_uq).reshape(B, S, MLA_HEADS, MLA_NOPE_DIM + MLA_ROPE_DIM)
    q_nope = q[..., :MLA_NOPE_DIM]
    q_rope = _rope(q[..., MLA_NOPE_DIM:], cos[:, :, None], sin[:, :, None])
    kv = (_rms_norm(c_kv, kv_norm_g) @ w_ukv).reshape(B, S, MLA_HEADS, MLA_NOPE_DIM + MLA_V_DIM)
    k_nope, v = kv[..., :MLA_NOPE_DIM], kv[..., MLA_NOPE_DIM:]
    k_r = _rope(k_rope, cos, sin)
    nb = S // MLA_BLOCK
    scale = (MLA_NOPE_DIM + MLA_ROPE_DIM) ** -0.5
    qn_b = q_nope.reshape(B, nb, MLA_BLOCK, MLA_HEADS, MLA_NOPE_DIM).transpose(1, 0, 2, 3, 4)
    qr_b = q_rope.reshape(B, nb, MLA_BLOCK, MLA_HEADS, MLA_ROPE_DIM).transpose(1, 0, 2, 3, 4)
    key_idx = jnp.arange(S)

    def block(args):
        qn, qr, i = args
        s = (jnp.einsum('bqhd,bkhd->bhqk', qn, k_nope, preferred_element_type=jnp.float32)
             + jnp.einsum('bqhr,bkr->bhqk', qr, k_r, preferred_element_type=jnp.float32)) * scale
        q_idx = i * MLA_BLOCK + jnp.arange(MLA_BLOCK)
        s = jnp.where((key_idx[None, :] <= q_idx[:, None])[None, None], s, MASK_VALUE)
        p = jax.nn.softmax(s, axis=-1).astype(v.dtype)
        return jnp.einsum('bhqk,bkhd->bqhd', p, v)

    out = lax.map(block, (qn_b, qr_b, jnp.arange(nb)))
    return out.transpose(1, 0, 2, 3, 4).reshape(B, S, B_OUT)


def _chunked_sgu(u, v, ln_g, ln_b, w_s, b_s):
    B, S = u.shape[:2]
    nc = S // SGU_CHUNK
    vn = _layer_norm(v, ln_g, ln_b).reshape(B, nc, SGU_CHUNK, SGU_GROUPS, SGU_GROUP_DIM)
    causal = jnp.tril(jnp.ones((SGU_CHUNK, SGU_CHUNK), dtype=bool))
    w = jnp.where(causal[None], w_s, 0.0)
    mixed = jnp.einsum('gts,bnsgc->bntgc', w, vn) + b_s.T[None, None, :, :, None]
    return u * mixed.reshape(B, S, SGU_WIDTH)


def setup_inputs(seed: int = 0) -> dict:
    key = jax.random.key(seed)
    ks = jax.random.split(key, 26)
    L, D = DEPTH, D_MODEL
    f32 = jnp.float32
    nrm = lambda k, shape, s: jax.random.normal(k, shape, f32) * s
    x = jax.random.normal(ks[0], (BATCH, SEQ, D), f32)
    offset = jax.random.randint(ks[1], (BATCH, 1), 0, 1024, dtype=jnp.int32)
    positions = (offset + jnp.arange(SEQ, dtype=jnp.int32)[None, :]).astype(jnp.int32)
    return {
        "x": x,
        "positions": positions,
        "w_in": nrm(ks[2], (L, D, N_IN), D ** -0.5),
        "b_gate": nrm(ks[3], (L, N_BRANCHES, D), 0.1),
        "sinks": nrm(ks[4], (L, SWA_Q_HEADS), 0.5),
        "q_norm_g": 1.0 + nrm(ks[5], (L, MLA_Q_RANK), 0.02),
        "kv_norm_g": 1.0 + nrm(ks[6], (L, MLA_KV_RANK), 0.02),
        "w_uq": nrm(ks[7], (L, MLA_Q_RANK, MLA_HEADS * (MLA_NOPE_DIM + MLA_ROPE_DIM)), MLA_Q_RANK ** -0.5),
        "w_ukv": nrm(ks[8], (L, MLA_KV_RANK, MLA_HEADS * (MLA_NOPE_DIM + MLA_V_DIM)), MLA_KV_RANK ** -0.5),
        "sgu_ln_g": 1.0 + nrm(ks[9], (L, SGU_WIDTH), 0.02),
        "sgu_ln_b": nrm(ks[10], (L, SGU_WIDTH), 0.02),
        "sgu_w": nrm(ks[11], (L, SGU_GROUPS, SGU_CHUNK, SGU_CHUNK), SGU_CHUNK ** -0.5),
        "sgu_b": 1.0 + nrm(ks[12], (L, SGU_GROUPS, SGU_CHUNK), 0.02),
        "w_proj_a": nrm(ks[13], (L, A_Q, D), A_Q ** -0.5),
        "w_proj_b": nrm(ks[14], (L, B_OUT, D), B_OUT ** -0.5),
        "w_proj_c": nrm(ks[15], (L, SGU_WIDTH, D), SGU_WIDTH ** -0.5),
        "w_o": nrm(ks[16], (L, D, D), DN_BETA * D ** -0.5),
        "ln1_g": 1.0 + nrm(ks[17], (L, D), 0.02),
        "ln1_b": nrm(ks[18], (L, D), 0.02),
        "w_up": nrm(ks[19], (L, D, 2 * D_FF), D ** -0.5),
        "conv_w": nrm(ks[20], (L, CONV_WIDTH, 2 * D_FF), CONV_WIDTH ** -0.5),
        "conv_b": nrm(ks[21], (L, 2 * D_FF), 0.02),
        "w_down": nrm(ks[22], (L, D_FF, D), DN_BETA * D_FF ** -0.5),
        "ln2_g": 1.0 + nrm(ks[23], (L, D), 0.02),
        "ln2_b": nrm(ks[24], (L, D), 0.02),
    }


def reference(x, positions, w_in, b_gate, sinks, q_norm_g, kv_norm_g, w_uq, w_ukv,
              sgu_ln_g, sgu_ln_b, sgu_w, sgu_b, w_proj_a, w_proj_b, w_proj_c, w_o,
              ln1_g, ln1_b, w_up, conv_w, conv_b, w_down, ln2_g, ln2_b):
    B, S, D = x.shape
    inv_freq = ROPE_THETA ** (-jnp.arange(0, MLA_ROPE_DIM, 2, dtype=jnp.float32) / MLA_ROPE_DIM)
    ang = positions.astype(jnp.float32)[..., None] * inv_freq
    cos, sin = jnp.cos(ang).astype(x.dtype), jnp.sin(ang).astype(x.dtype)
    split_at = [A_Q, A_Q + A_KV, A_Q + 2 * A_KV]
    split_at += [split_at[-1] + MLA_Q_RANK]
    split_at += [split_at[-1] + MLA_KV_RANK]
    split_at += [split_at[-1] + MLA_ROPE_DIM]
    split_at += [split_at[-1] + SGU_WIDTH]
    split_at += [split_at[-1] + SGU_WIDTH]

    for l in range(DEPTH):
        h = x @ w_in[l]
        qa, ka, va, c_q, c_kv, k_rope, hu, hv, g_logit = jnp.split(h, split_at, axis=-1)
        y_a = _sliding_window_gqa(qa, ka, va, sinks[l])
        y_b = _mla(c_q, c_kv, k_rope, cos, sin, q_norm_g[l], kv_norm_g[l], w_uq[l], w_ukv[l])
        y_c = _chunked_sgu(jax.nn.gelu(hu, approximate=False), jax.nn.gelu(hv, approximate=False),
                           sgu_ln_g[l], sgu_ln_b[l], sgu_w[l], sgu_b[l])
        gates = jax.nn.sigmoid((g_logit.reshape(B, S, N_BRANCHES, D) + b_gate[l]).astype(jnp.float32)).astype(x.dtype)
        merged = (gates[:, :, 0] * (y_a @ w_proj_a[l])
                  + gates[:, :, 1] * (y_b @ w_proj_b[l])
                  + gates[:, :, 2] * (y_c @ w_proj_c[l]))
        x = _layer_norm(DN_ALPHA * x + merged @ w_o[l], ln1_g[l], ln1_b[l])

        up = x @ w_up[l]
        up_pad = jnp.pad(up, ((0, 0), (CONV_WIDTH - 1, 0), (0, 0)))
        conv = conv_b[l] + sum(up_pad[:, j:j + S] * conv_w[l, j] for j in range(CONV_WIDTH))
        gate, val = jnp.split(conv, 2, axis=-1)
        x = _layer_norm(DN_ALPHA * x + (jax.nn.silu(gate) * val) @ w_down[l], ln2_g[l], ln2_b[l])
    return x
```

```python
import functools
import math

import jax
import jax.numpy as jnp
from jax import lax
from jax.experimental import pallas as pl
from jax.experimental.pallas import tpu as pltpu

F32 = jnp.float32
BF16 = jnp.bfloat16

D_MODEL = 2048
DEPTH = 2
SWA_Q_HEADS = 16
SWA_KV_HEADS = 2
SWA_HEAD_DIM = 64
SWA_BLOCK = 128
MLA_HEADS = 16
MLA_Q_RANK = 512
MLA_KV_RANK = 512
MLA_NOPE_DIM = 128
MLA_ROPE_DIM = 64
MLA_V_DIM = 128
ROPE_THETA = 10000.0
SGU_GROUPS = 8
SGU_GROUP_DIM = 128
SGU_CHUNK = 128
SGU_WIDTH = SGU_GROUPS * SGU_GROUP_DIM
D_FF = 5632
CONV_WIDTH = 3
N_BRANCHES = 3
EPS = 1e-5
MASK_VALUE = -1e30
DN_ALPHA = (2 * DEPTH) ** 0.25

A_Q = SWA_Q_HEADS * SWA_HEAD_DIM
A_KV = SWA_KV_HEADS * SWA_HEAD_DIM
B_OUT = MLA_HEADS * MLA_V_DIM
MLA_QK_PAD = 256

LANES = 128
VMEM_LIMIT = 56 * 1024 * 1024

H_G = 0
H_QA = H_G + N_BRANCHES * D_MODEL
H_HU = H_QA + A_Q
H_HV = H_HU + SGU_WIDTH
H_CQ = H_HV + SGU_WIDTH
H_CKV = H_CQ + MLA_Q_RANK
H_KA = H_CKV + MLA_KV_RANK
H_VA = H_KA + A_KV
H_KR = H_VA + A_KV
H_WIDTH = 10752


def _cparams(sem):
    return pltpu.CompilerParams(dimension_semantics=sem, vmem_limit_bytes=VMEM_LIMIT)


def _layer_norm_rows(y, g, b):
    mu = jnp.mean(y, axis=-1, keepdims=True)
    yc = y - mu
    var = jnp.mean(yc * yc, axis=-1, keepdims=True)
    return yc * lax.rsqrt(var + EPS) * g + b


def _rope_table_kernel(pos_ref, cos_ref, sin_ref):
    lane = lax.broadcasted_iota(jnp.int32, (1, LANES), 1)
    k = (lane % (MLA_ROPE_DIM // 2)).astype(F32)
    inv_freq = jnp.exp(k * (-2.0 * math.log(ROPE_THETA) / MLA_ROPE_DIM))
    ang = pos_ref[...].astype(F32) * inv_freq
    c, s = jnp.cos(ang), jnp.sin(ang)
    lane_b = lax.broadcasted_iota(jnp.int32, ang.shape, 1)
    cos_ref[...] = jnp.where(lane_b < MLA_ROPE_DIM, c, 0.0)
    sin_ref[...] = jnp.where(lane_b < MLA_ROPE_DIM // 2, -s,
                             jnp.where(lane_b < MLA_ROPE_DIM, s, 0.0))


def _rope_tables(pos_col):
    T = pos_col.shape[0]
    tm = 2048
    return pl.pallas_call(
        _rope_table_kernel,
        grid=(T // tm,),
        in_specs=[pl.BlockSpec((tm, 1), lambda i: (i, 0))],
        out_specs=[pl.BlockSpec((tm, LANES), lambda i: (i, 0))] * 2,
        out_shape=[jax.ShapeDtypeStruct((T, LANES), F32)] * 2,
        compiler_params=_cparams(("arbitrary",)),
        name="rope_tables",
    )(pos_col)


def _rope_block(x, cosm, sinm):
    lane = lax.broadcasted_iota(jnp.int32, x.shape, 1)
    swapped = jnp.where(lane < MLA_ROPE_DIM // 2,
                        pltpu.roll(x, LANES - MLA_ROPE_DIM // 2, axis=1),
                        pltpu.roll(x, MLA_ROPE_DIM // 2, axis=1))
    return x * cosm + swapped * sinm


def _inproj_kernel(x_ref, w_ref, o_ref, xb_ref):
    @pl.when(pl.program_id(1) == 0)
    def _():
        xb_ref[...] = x_ref[...].astype(BF16)
    o_ref[...] = jnp.dot(xb_ref[...], w_ref[...], preferred_element_type=F32).astype(o_ref.dtype)


def _inproj(x2d, w):
    T, K = x2d.shape
    N = w.shape[1]
    tm, tn = 1024, 1536
    return pl.pallas_call(
        _inproj_kernel,
        grid=(T // tm, N // tn),
        in_specs=[pl.BlockSpec((tm, K), lambda i, j: (i, 0)),
                  pl.BlockSpec((K, tn), lambda i, j: (0, j))],
        out_specs=pl.BlockSpec((tm, tn), lambda i, j: (i, j)),
        out_shape=jax.ShapeDtypeStruct((T, N), BF16),
        scratch_shapes=[pltpu.VMEM((tm, K), BF16)],
        compiler_params=_cparams(("arbitrary", "arbitrary")),
        name="inproj",
    )(x2d, w)


def _swa_kernel(sinks_ref, q_ref, kp_ref, kc_ref, vp_ref, vc_ref, o_ref):
    n = pl.program_id(1)
    blk = SWA_BLOCK
    half = SWA_HEAD_DIM
    k = jnp.concatenate([kp_ref[...], kc_ref[...]], axis=0).astype(F32)
    v = jnp.concatenate([vp_ref[...], vc_ref[...]], axis=0).astype(F32)
    lo = lax.broadcasted_iota(jnp.int32, k.shape, 1) < half
    k_sw, v_sw = pltpu.roll(k, half, axis=1), pltpu.roll(v, half, axis=1)
    zero = jnp.zeros_like(k)
    k_lo = [jnp.where(lo, k, zero).astype(BF16), jnp.where(lo, k_sw, zero).astype(BF16)]
    k_hi = [jnp.where(lo, zero, k_sw).astype(BF16), jnp.where(lo, zero, k).astype(BF16)]
    v_lo = [jnp.where(lo, v, zero).astype(BF16), jnp.where(lo, v_sw, zero).astype(BF16)]
    v_hi = [jnp.where(lo, zero, v_sw).astype(BF16), jnp.where(lo, zero, v).astype(BF16)]

    row = lax.broadcasted_iota(jnp.int32, (blk, 2 * blk), 0)
    col = lax.broadcasted_iota(jnp.int32, (blk, 2 * blk), 1)
    first_key = jnp.where(n > 0, 0, blk)
    valid = (col > row) & (col <= row + blk) & (col >= first_key)

    group = SWA_Q_HEADS // SWA_KV_HEADS
    pairs = group // 2
    gw = group * SWA_HEAD_DIM
    for h in range(SWA_KV_HEADS):
        qs = jnp.concatenate(
            [q_ref[:, h * gw + j * LANES: h * gw + (j + 1) * LANES] for j in range(pairs)], axis=0)
        acc = None
        for hf, kk, vv in ((0, k_lo[h], v_lo[h]), (1, k_hi[h], v_hi[h])):
            s = lax.dot_general(qs, kk, (((1,), (1,)), ((), ())),
                                preferred_element_type=F32) * (SWA_HEAD_DIM ** -0.5)
            ps = []
            for j in range(pairs):
                sink = sinks_ref[h * group + 2 * j + hf]
                sj = jnp.where(valid, s[j * blk:(j + 1) * blk], MASK_VALUE)
                m = jnp.maximum(jnp.max(sj, axis=-1, keepdims=True), sink)
                p = jnp.exp(sj - m)
                den = jnp.sum(p, axis=-1, keepdims=True) + jnp.exp(sink - m)
                ps.append((p / den).astype(BF16))
            contrib = jnp.dot(jnp.concatenate(ps, axis=0), vv, preferred_element_type=F32)
            acc = contrib if acc is None else acc + contrib
        for j in range(pairs):
            o_ref[:, h * gw + j * LANES: h * gw + (j + 1) * LANES] = (
                acc[j * blk:(j + 1) * blk].astype(o_ref.dtype))


def _swa(h, sinks, batch, seq):
    T = h.shape[0]
    nb = seq // SWA_BLOCK
    cur = lambda c: (lambda b, n: (b * nb + n, c))
    prev = lambda c: (lambda b, n: (b * nb + jnp.maximum(n - 1, 0), c))
    kv_spec = lambda f, c: pl.BlockSpec((SWA_BLOCK, A_KV), f(c))
    return pl.pallas_call(
        _swa_kernel,
        grid=(batch, nb),
        in_specs=[pl.BlockSpec(memory_space=pltpu.SMEM),
                  pl.BlockSpec((SWA_BLOCK, A_Q), cur(H_QA // A_Q)),
                  kv_spec(prev, H_KA // A_KV), kv_spec(cur, H_KA // A_KV),
                  kv_spec(prev, H_VA // A_KV), kv_spec(cur, H_VA // A_KV)],
        out_specs=pl.BlockSpec((SWA_BLOCK, A_Q), lambda b, n: (b * nb + n, 0)),
        out_shape=jax.ShapeDtypeStruct((T, A_Q), BF16),
        compiler_params=_cparams(("arbitrary", "arbitrary")),
        name="swa",
    )(sinks, h, h, h, h, h)


def _rms_rows(x, g):
    return x * lax.rsqrt(jnp.mean(x * x, axis=-1, keepdims=True) + EPS) * g


def _mla_prep_kernel(cq_ref, ckv_ref, kr_ref, cos_ref, sin_ref, qg_ref, kvg_ref,
                     wq_ref, wk_ref, wv_ref, q_ref, k_ref, v_ref):
    scale = (MLA_NOPE_DIM + MLA_ROPE_DIM) ** -0.5
    cosm, sinm = cos_ref[...], sin_ref[...]
    cqn = _rms_rows(cq_ref[...].astype(F32), qg_ref[...]).astype(BF16)
    ckvn = _rms_rows(ckv_ref[...].astype(F32), kvg_ref[...]).astype(BF16)
    q = jnp.dot(cqn, wq_ref[...], preferred_element_type=F32)
    kn = jnp.dot(ckvn, wk_ref[...], preferred_element_type=F32)
    v_ref[...] = jnp.dot(ckvn, wv_ref[...], preferred_element_type=F32).astype(v_ref.dtype)
    kr = _rope_block(kr_ref[...].astype(F32), cosm, sinm).astype(k_ref.dtype)
    for h in range(MLA_HEADS):
        b = h * MLA_QK_PAD
        q_ref[:, b:b + LANES] = (q[:, b:b + LANES] * scale).astype(q_ref.dtype)
        q_ref[:, b + LANES:b + 2 * LANES] = (
            _rope_block(q[:, b + LANES:b + 2 * LANES], cosm, sinm) * scale).astype(q_ref.dtype)
        k_ref[:, b:b + LANES] = kn[:, h * LANES:(h + 1) * LANES].astype(k_ref.dtype)
        k_ref[:, b + LANES:b + 2 * LANES] = kr


def _mla_prep(h, cosm, sinm, qg, kvg, wq, wk, wv):
    T = h.shape[0]
    tm = 512
    row = lambda c: (lambda i: (i, c))
    full = lambda a: pl.BlockSpec(a.shape, lambda i: (0, 0))
    return pl.pallas_call(
        _mla_prep_kernel,
        grid=(T // tm,),
        in_specs=[pl.BlockSpec((tm, MLA_Q_RANK), row(H_CQ // MLA_Q_RANK)),
                  pl.BlockSpec((tm, MLA_KV_RANK), row(H_CKV // MLA_KV_RANK)),
                  pl.BlockSpec((tm, LANES), row(H_KR // LANES)),
                  pl.BlockSpec((tm, LANES), row(0)), pl.BlockSpec((tm, LANES), row(0)),
                  full(qg), full(kvg), full(wq), full(wk), full(wv)],
        out_specs=[pl.BlockSpec((tm, MLA_HEADS * MLA_QK_PAD), row(0)),
                   pl.BlockSpec((tm, MLA_HEADS * MLA_QK_PAD), row(0)),
                   pl.BlockSpec((tm, B_OUT), row(0))],
        out_shape=[jax.ShapeDtypeStruct((T, MLA_HEADS * MLA_QK_PAD), BF16),
                   jax.ShapeDtypeStruct((T, MLA_HEADS * MLA_QK_PAD), BF16),
                   jax.ShapeDtypeStruct((T, B_OUT), BF16)],
        compiler_params=_cparams(("arbitrary",)),
        name="mla_prep",
    )(h, h, h, cosm, sinm, qg, kvg, wq, wk, wv)


def _mla_attn_kernel(q_ref, k_ref, v_ref, o_ref, *, tq):
    i = pl.program_id(2)
    q = q_ref[...]

    def chunk(j, carry, diagonal):
        m, l, acc = carry
        start = pl.multiple_of(j * tq, tq)
        kj = k_ref[pl.ds(start, tq), :]
        vj = v_ref[pl.ds(start, tq), :]
        s = lax.dot_general(q, kj, (((1,), (1,)), ((), ())), preferred_element_type=F32)
        if diagonal:
            row = lax.broadcasted_iota(jnp.int32, s.shape, 0)
            col = lax.broadcasted_iota(jnp.int32, s.shape, 1)
            s = jnp.where(col <= row, s, MASK_VALUE)
        m_new = jnp.maximum(m, jnp.max(s, axis=-1, keepdims=True))
        alpha = jnp.exp(m - m_new)
        p = jnp.exp(s - m_new)
        l = alpha * l + jnp.sum(p, axis=-1, keepdims=True)
        acc = alpha * acc + jnp.dot(p.astype(BF16), vj, preferred_element_type=F32)
        return m_new, l, acc

    init = (jnp.full((tq, 1), MASK_VALUE, F32), jnp.zeros((tq, 1), F32),
            jnp.zeros((tq, MLA_V_DIM), F32))
    carry = lax.fori_loop(0, i, lambda j, c: chunk(j, c, False), init)
    _, l, acc = chunk(i, carry, True)
    o_ref[...] = (acc / l).astype(o_ref.dtype)


def _mla_attn(q, k, v, batch, seq):
    T = q.shape[0]
    tq = 512
    nq = seq // tq
    return pl.pallas_call(
        functools.partial(_mla_attn_kernel, tq=tq),
        grid=(batch, MLA_HEADS, nq),
        in_specs=[pl.BlockSpec((tq, MLA_QK_PAD), lambda b, h, i: (b * nq + i, h)),
                  pl.BlockSpec((seq, MLA_QK_PAD), lambda b, h, i: (b, h)),
                  pl.BlockSpec((seq, MLA_V_DIM), lambda b, h, i: (b, h))],
        out_specs=pl.BlockSpec((tq, MLA_V_DIM), lambda b, h, i: (b * nq + i, h)),
        out_shape=jax.ShapeDtypeStruct((T, B_OUT), BF16),
        compiler_params=_cparams(("arbitrary", "arbitrary", "arbitrary")),
        name="mla_attn",
    )(q, k, v)


def _gelu_erf(x):
    return 0.5 * x * (1.0 + lax.erf(x * (2.0 ** -0.5)))


def _sgu_kernel(hu_ref, hv_ref, g_ref, b_ref, w_ref, bt_ref, o_ref):
    u = _gelu_erf(hu_ref[...].astype(F32))
    v = _gelu_erf(hv_ref[...].astype(F32))
    vn = _layer_norm_rows(v, g_ref[...], b_ref[...]).astype(BF16)
    tm = vn.shape[0]
    r = lax.broadcasted_iota(jnp.int32, (SGU_CHUNK, SGU_CHUNK), 0)
    c = lax.broadcasted_iota(jnp.int32, (SGU_CHUNK, SGU_CHUNK), 1)
    causal = c <= r
    for g in range(SGU_GROUPS):
        w = jnp.where(causal, w_ref[g], 0.0).astype(BF16)
        bias = bt_ref[:, g:g + 1]
        cols = slice(g * SGU_GROUP_DIM, (g + 1) * SGU_GROUP_DIM)
        for n in range(tm // SGU_CHUNK):
            rows = slice(n * SGU_CHUNK, (n + 1) * SGU_CHUNK)
            mixed = jnp.dot(w, vn[rows, cols], preferred_element_type=F32) + bias
            o_ref[rows, cols] = (u[rows, cols] * mixed).astype(o_ref.dtype)


def _sgu(h, ln_g, ln_b, w_s, b_t):
    T = h.shape[0]
    tm = 512
    full2 = lambda a: pl.BlockSpec(a.shape, lambda i: (0,) * a.ndim)
    return pl.pallas_call(
        _sgu_kernel,
        grid=(T // tm,),
        in_specs=[pl.BlockSpec((tm, SGU_WIDTH), lambda i: (i, H_HU // SGU_WIDTH)),
                  pl.BlockSpec((tm, SGU_WIDTH), lambda i: (i, H_HV // SGU_WIDTH)),
                  full2(ln_g), full2(ln_b), full2(w_s), full2(b_t)],
        out_specs=pl.BlockSpec((tm, SGU_WIDTH), lambda i: (i, 0)),
        out_shape=jax.ShapeDtypeStruct((T, SGU_WIDTH), BF16),
        compiler_params=_cparams(("arbitrary",)),
        name="sgu",
    )(h, h, ln_g, ln_b, w_s, b_t)


def _merge_kernel(x_ref, ya_ref, yb_ref, yc_ref, gl_ref, bg_ref, pa_ref, pb_ref, pc_ref,
                  wo_ref, g_ref, b_ref, o_ref):
    merged = None
    for br, (y_ref, p_ref) in enumerate(((ya_ref, pa_ref), (yb_ref, pb_ref), (yc_ref, pc_ref))):
        cols = slice(br * D_MODEL, (br + 1) * D_MODEL)
        gate = jax.nn.sigmoid(gl_ref[:, cols].astype(F32) + bg_ref[br:br + 1, :])
        term = gate * jnp.dot(y_ref[...], p_ref[...], preferred_element_type=F32)
        merged = term if merged is None else merged + term
    y = DN_ALPHA * x_ref[...] + jnp.dot(merged.astype(BF16), wo_ref[...],
                                        preferred_element_type=F32)
    o_ref[...] = _layer_norm_rows(y, g_ref[...], b_ref[...])


def _merge(x2d, ya, yb, yc, h, b_gate, pa, pb, pc, wo, ln_g, ln_b):
    T = x2d.shape[0]
    tm = 256
    row = lambda a_w: pl.BlockSpec((tm, a_w), lambda i: (i, 0))
    res = lambda a: pl.BlockSpec(a.shape, lambda i: (0, 0), pipeline_mode=pl.Buffered(1))
    return pl.pallas_call(
        _merge_kernel,
        grid=(T // tm,),
        in_specs=[row(D_MODEL), row(A_Q), row(B_OUT), row(SGU_WIDTH),
                  row(N_BRANCHES * D_MODEL), res(b_gate), res(pa), res(pb), res(pc), res(wo),
                  res(ln_g), res(ln_b)],
        out_specs=row(D_MODEL),
        out_shape=jax.ShapeDtypeStruct((T, D_MODEL), F32),
        compiler_params=_cparams(("arbitrary",)),
        name="merge",
    )(x2d, ya, yb, yc, h, b_gate, pa, pb, pc, wo, ln_g, ln_b)


FFN_HALO = 8


def _causal_conv(u, prev, cw, cb):
    tm = u.shape[0]
    ext = jnp.concatenate([prev, u], axis=0)
    u1 = pltpu.roll(ext, 1, axis=0)[FFN_HALO:FFN_HALO + tm]
    u2 = pltpu.roll(ext, 2, axis=0)[FFN_HALO:FFN_HALO + tm]
    return cb + cw[0:1] * u2 + cw[1:2] * u1 + cw[2:3] * u


def _ffn_kernel(x_ref, wg_ref, wv_ref, cwg_ref, cwv_ref, cbg_ref, cbv_ref, wd_ref, g_ref, b_ref,
                o_ref, xb_ref, acc_ref, halo_ref, *, tiles_per_seq):
    i, j = pl.program_id(0), pl.program_id(1)
    tm = x_ref.shape[0]

    @pl.when(j == 0)
    def _():
        xb_ref[...] = x_ref[...].astype(BF16)
        acc_ref[...] = jnp.zeros_like(acc_ref)

    @pl.when(i % tiles_per_seq == 0)
    def _():
        halo_ref[j] = jnp.zeros(halo_ref.shape[1:], F32)

    xb = xb_ref[...]
    ug = jnp.dot(xb, wg_ref[...], preferred_element_type=F32)
    uv = jnp.dot(xb, wv_ref[...], preferred_element_type=F32)
    fc = ug.shape[1]
    halo = halo_ref[j]
    gate = _causal_conv(ug, halo[:, :fc], cwg_ref[...], cbg_ref[...])
    val = _causal_conv(uv, halo[:, fc:], cwv_ref[...], cbv_ref[...])
    halo_ref[j] = jnp.concatenate([ug[tm - FFN_HALO:], uv[tm - FFN_HALO:]], axis=1)
    act = (gate * jax.nn.sigmoid(gate) * val).astype(BF16)
    acc_ref[...] += jnp.dot(act, wd_ref[...], preferred_element_type=F32)

    @pl.when(j == pl.num_programs(1) - 1)
    def _():
        y = DN_ALPHA * x_ref[...] + acc_ref[...]
        o_ref[...] = _layer_norm_rows(y, g_ref[...], b_ref[...])


def _ffn(x2d, w_up, conv_w, conv_b, w_down, ln_g, ln_b, seq):
    T = x2d.shape[0]
    tm, fc = 512, 512
    nf = D_FF // fc
    gate_c = lambda i, j: (0, j)
    val_c = lambda i, j: (0, nf + j)
    const = lambda i, j: (0, 0)
    return pl.pallas_call(
        functools.partial(_ffn_kernel, tiles_per_seq=seq // tm),
        grid=(T // tm, nf),
        in_specs=[pl.BlockSpec((tm, D_MODEL), lambda i, j: (i, 0)),
                  pl.BlockSpec((D_MODEL, fc), gate_c), pl.BlockSpec((D_MODEL, fc), val_c),
                  pl.BlockSpec((CONV_WIDTH, fc), gate_c), pl.BlockSpec((CONV_WIDTH, fc), val_c),
                  pl.BlockSpec((1, fc), gate_c), pl.BlockSpec((1, fc), val_c),
                  pl.BlockSpec((fc, D_MODEL), lambda i, j: (j, 0)),
                  pl.BlockSpec((1, D_MODEL), const), pl.BlockSpec((1, D_MODEL), const)],
        out_specs=pl.BlockSpec((tm, D_MODEL), lambda i, j: (i, 0)),
        out_shape=jax.ShapeDtypeStruct((T, D_MODEL), F32),
        scratch_shapes=[pltpu.VMEM((tm, D_MODEL), BF16), pltpu.VMEM((tm, D_MODEL), F32),
                        pltpu.VMEM((nf, FFN_HALO, 2 * fc), F32)],
        compiler_params=_cparams(("arbitrary", "arbitrary")),
        name="ffn",
    )(x2d, w_up, w_up, conv_w, conv_w, conv_b, conv_b, w_down, ln_g, ln_b)


def _reorder_w_in(w):
    o = 0
    qa = w[:, o:o + A_Q]; o += A_Q
    ka = w[:, o:o + A_KV]; o += A_KV
    va = w[:, o:o + A_KV]; o += A_KV
    cq = w[:, o:o + MLA_Q_RANK]; o += MLA_Q_RANK
    ckv = w[:, o:o + MLA_KV_RANK]; o += MLA_KV_RANK
    kr = w[:, o:o + MLA_ROPE_DIM]; o += MLA_ROPE_DIM
    hu = w[:, o:o + SGU_WIDTH]; o += SGU_WIDTH
    hv = w[:, o:o + SGU_WIDTH]; o += SGU_WIDTH
    g = w[:, o:]
    pad = jnp.zeros((w.shape[0], H_WIDTH - H_KR - MLA_ROPE_DIM), w.dtype)
    return jnp.concatenate([g, qa, hu, hv, cq, ckv, ka, va, kr, pad], axis=1).astype(BF16)


def _pad_w_uq(w):
    r = w.shape[0]
    w = w.reshape(r, MLA_HEADS, MLA_NOPE_DIM + MLA_ROPE_DIM)
    w = jnp.pad(w, ((0, 0), (0, 0), (0, MLA_QK_PAD - MLA_NOPE_DIM - MLA_ROPE_DIM)))
    return w.reshape(r, MLA_HEADS * MLA_QK_PAD).astype(BF16)


def _split_w_ukv(w):
    r = w.shape[0]
    w = w.reshape(r, MLA_HEADS, MLA_NOPE_DIM + MLA_V_DIM)
    wk = w[:, :, :MLA_NOPE_DIM].reshape(r, MLA_HEADS * MLA_NOPE_DIM)
    wv = w[:, :, MLA_NOPE_DIM:].reshape(r, MLA_HEADS * MLA_V_DIM)
    return wk.astype(BF16), wv.astype(BF16)


def kernel(x, positions, w_in, b_gate, sinks, q_norm_g, kv_norm_g, w_uq, w_ukv, sgu_ln_g, sgu_ln_b, sgu_w, sgu_b, w_proj_a, w_proj_b, w_proj_c, w_o, ln1_g, ln1_b, w_up, conv_w, conv_b, w_down, ln2_g, ln2_b):
    B, S, D = x.shape
    T = B * S
    x2d = x.reshape(T, D)
    cosm, sinm = _rope_tables(positions.reshape(T, 1))
    row = lambda a: a.reshape(1, -1)
    for l in range(DEPTH):
        h = _inproj(x2d, _reorder_w_in(w_in[l]))
        y_a = _swa(h, sinks[l], B, S)
        wk, wv = _split_w_ukv(w_ukv[l])
        q, k, v = _mla_prep(h, cosm, sinm, row(q_norm_g[l]), row(kv_norm_g[l]),
                            _pad_w_uq(w_uq[l]), wk, wv)
        y_b = _mla_attn(q, k, v, B, S)
        y_c = _sgu(h, row(sgu_ln_g[l]), row(sgu_ln_b[l]), sgu_w[l], sgu_b[l].T)
        x2d = _merge(x2d, y_a, y_b, y_c, h, b_gate[l], w_proj_a[l].astype(BF16),
                     w_proj_b[l].astype(BF16), w_proj_c[l].astype(BF16), w_o[l].astype(BF16),
                     row(ln1_g[l]), row(ln1_b[l]))
        x2d = _ffn(x2d, w_up[l].astype(BF16), conv_w[l], row(conv_b[l]), w_down[l].astype(BF16),
                   row(ln2_g[l]), row(ln2_b[l]), S)
    return x2d.reshape(B, S, D)
```

```python
import functools
import math

import jax
import jax.numpy as jnp
from jax import lax
from jax.experimental import pallas as pl
from jax.experimental.pallas import tpu as pltpu

F32 = jnp.float32
BF16 = jnp.bfloat16

D_MODEL = 2048
DEPTH = 2
SWA_Q_HEADS = 16
SWA_KV_HEADS = 2
SWA_HEAD_DIM = 64
SWA_BLOCK = 128
MLA_HEADS = 16
MLA_Q_RANK = 512
MLA_KV_RANK = 512
MLA_NOPE_DIM = 128
MLA_ROPE_DIM = 64
MLA_V_DIM = 128
ROPE_THETA = 10000.0
SGU_GROUPS = 8
SGU_GROUP_DIM = 128
SGU_CHUNK = 128
SGU_WIDTH = SGU_GROUPS * SGU_GROUP_DIM
D_FF = 5632
CONV_WIDTH = 3
N_BRANCHES = 3
EPS = 1e-5
MASK_VALUE = -1e30
DN_ALPHA = (2 * DEPTH) ** 0.25

A_Q = SWA_Q_HEADS * SWA_HEAD_DIM
A_KV = SWA_KV_HEADS * SWA_HEAD_DIM
B_OUT = MLA_HEADS * MLA_V_DIM
MLA_QK_PAD = 256

LANES = 128
VMEM_LIMIT = 56 * 1024 * 1024

H_G = 0
H_QA = H_G + N_BRANCHES * D_MODEL
H_HU = H_QA + A_Q
H_HV = H_HU + SGU_WIDTH
H_CQ = H_HV + SGU_WIDTH
H_CKV = H_CQ + MLA_Q_RANK
H_KA = H_CKV + MLA_KV_RANK
H_VA = H_KA + A_KV
H_KR = H_VA + A_KV
H_WIDTH = 10752


def _cparams(sem):
    return pltpu.CompilerParams(dimension_semantics=sem, vmem_limit_bytes=VMEM_LIMIT)


def _layer_norm_rows(y, g, b):
    mu = jnp.mean(y, axis=-1, keepdims=True)
    yc = y - mu
    var = jnp.mean(yc * yc, axis=-1, keepdims=True)
    return yc * lax.rsqrt(var + EPS) * g + b


def _rope_table_kernel(pos_ref, cos_ref, sin_ref):
    lane = lax.broadcasted_iota(jnp.int32, (1, LANES), 1)
    k = (lane % (MLA_ROPE_DIM // 2)).astype(F32)
    inv_freq = jnp.exp(k * (-2.0 * math.log(ROPE_THETA) / MLA_ROPE_DIM))
    ang = pos_ref[...].astype(F32) * inv_freq
    c, s = jnp.cos(ang), jnp.sin(ang)
    lane_b = lax.broadcasted_iota(jnp.int32, ang.shape, 1)
    cos_ref[...] = jnp.where(lane_b < MLA_ROPE_DIM, c, 0.0)
    sin_ref[...] = jnp.where(lane_b < MLA_ROPE_DIM // 2, -s,
                             jnp.where(lane_b < MLA_ROPE_DIM, s, 0.0))


def _rope_tables(pos_col):
    T = pos_col.shape[0]
    tm = 2048
    return pl.pallas_call(
        _rope_table_kernel,
        grid=(T // tm,),
        in_specs=[pl.BlockSpec((tm, 1), lambda i: (i, 0))],
        out_specs=[pl.BlockSpec((tm, LANES), lambda i: (i, 0))] * 2,
        out_shape=[jax.ShapeDtypeStruct((T, LANES), F32)] * 2,
        compiler_params=_cparams(("arbitrary",)),
        name="rope_tables",
    )(pos_col)


def _rope_block(x, cosm, sinm):
    lane = lax.broadcasted_iota(jnp.int32, x.shape, 1)
    swapped = jnp.where(lane < MLA_ROPE_DIM // 2,
                        pltpu.roll(x, LANES - MLA_ROPE_DIM // 2, axis=1),
                        pltpu.roll(x, MLA_ROPE_DIM // 2, axis=1))
    return x * cosm + swapped * sinm


def _inproj_kernel(x_ref, w_ref, o_ref, xb_ref):
    @pl.when(pl.program_id(1) == 0)
    def _():
        xb_ref[...] = x_ref[...].astype(BF16)
    o_ref[...] = jnp.dot(xb_ref[...], w_ref[...], preferred_element_type=F32).astype(o_ref.dtype)


def _layer_spec(a, l):
    return pl.BlockSpec((None,) + a.shape[1:], lambda *_: (l,) + (0,) * (a.ndim - 1))


def _inproj(x2d, w, l):
    T, K = x2d.shape
    N = w.shape[2]
    tm, tn = 1024, 1536
    return pl.pallas_call(
        _inproj_kernel,
        grid=(T // tm, N // tn),
        in_specs=[pl.BlockSpec((tm, K), lambda i, j: (i, 0)),
                  pl.BlockSpec((None, K, tn), lambda i, j: (l, 0, j))],
        out_specs=pl.BlockSpec((tm, tn), lambda i, j: (i, j)),
        out_shape=jax.ShapeDtypeStruct((T, N), BF16),
        scratch_shapes=[pltpu.VMEM((tm, K), BF16)],
        compiler_params=_cparams(("arbitrary", "arbitrary")),
        name="inproj",
    )(x2d, w)


def _swa_kernel(sinks_ref, q_ref, kp_ref, kc_ref, vp_ref, vc_ref, o_ref, *, layer):
    n = pl.program_id(1)
    blk = SWA_BLOCK
    half = SWA_HEAD_DIM
    k = jnp.concatenate([kp_ref[...], kc_ref[...]], axis=0).astype(F32)
    v = jnp.concatenate([vp_ref[...], vc_ref[...]], axis=0).astype(F32)
    lo = lax.broadcasted_iota(jnp.int32, k.shape, 1) < half
    k_sw, v_sw = pltpu.roll(k, half, axis=1), pltpu.roll(v, half, axis=1)
    zero = jnp.zeros_like(k)
    k_lo = [jnp.where(lo, k, zero).astype(BF16), jnp.where(lo, k_sw, zero).astype(BF16)]
    k_hi = [jnp.where(lo, zero, k_sw).astype(BF16), jnp.where(lo, zero, k).astype(BF16)]
    v_lo = [jnp.where(lo, v, zero).astype(BF16), jnp.where(lo, v_sw, zero).astype(BF16)]
    v_hi = [jnp.where(lo, zero, v_sw).astype(BF16), jnp.where(lo, zero, v).astype(BF16)]

    row = lax.broadcasted_iota(jnp.int32, (blk, 2 * blk), 0)
    col = lax.broadcasted_iota(jnp.int32, (blk, 2 * blk), 1)
    first_key = jnp.where(n > 0, 0, blk)
    valid = (col > row) & (col <= row + blk) & (col >= first_key)

    group = SWA_Q_HEADS // SWA_KV_HEADS
    pairs = group // 2
    gw = group * SWA_HEAD_DIM
    for h in range(SWA_KV_HEADS):
        qs = jnp.concatenate(
            [q_ref[:, h * gw + j * LANES: h * gw + (j + 1) * LANES] for j in range(pairs)], axis=0)
        acc = None
        for hf, kk, vv in ((0, k_lo[h], v_lo[h]), (1, k_hi[h], v_hi[h])):
            s = lax.dot_general(qs, kk, (((1,), (1,)), ((), ())),
                                preferred_element_type=F32) * (SWA_HEAD_DIM ** -0.5)
            ps = []
            for j in range(pairs):
                sink = sinks_ref[layer, h * group + 2 * j + hf]
                sj = jnp.where(valid, s[j * blk:(j + 1) * blk], MASK_VALUE)
                m = jnp.maximum(jnp.max(sj, axis=-1, keepdims=True), sink)
                p = jnp.exp(sj - m)
                den = jnp.sum(p, axis=-1, keepdims=True) + jnp.exp(sink - m)
                ps.append((p / den).astype(BF16))
            contrib = jnp.dot(jnp.concatenate(ps, axis=0), vv, preferred_element_type=F32)
            acc = contrib if acc is None else acc + contrib
        for j in range(pairs):
            o_ref[:, h * gw + j * LANES: h * gw + (j + 1) * LANES] = (
                acc[j * blk:(j + 1) * blk].astype(o_ref.dtype))


def _swa(h, sinks, l, batch, seq):
    T = h.shape[0]
    nb = seq // SWA_BLOCK
    cur = lambda c: (lambda b, n: (b * nb + n, c))
    prev = lambda c: (lambda b, n: (b * nb + jnp.maximum(n - 1, 0), c))
    kv_spec = lambda f, c: pl.BlockSpec((SWA_BLOCK, A_KV), f(c))
    return pl.pallas_call(
        functools.partial(_swa_kernel, layer=l),
        grid=(batch, nb),
        in_specs=[pl.BlockSpec(memory_space=pltpu.SMEM),
                  pl.BlockSpec((SWA_BLOCK, A_Q), cur(H_QA // A_Q)),
                  kv_spec(prev, H_KA // A_KV), kv_spec(cur, H_KA // A_KV),
                  kv_spec(prev, H_VA // A_KV), kv_spec(cur, H_VA // A_KV)],
        out_specs=pl.BlockSpec((SWA_BLOCK, A_Q), lambda b, n: (b * nb + n, 0)),
        out_shape=jax.ShapeDtypeStruct((T, A_Q), BF16),
        compiler_params=_cparams(("arbitrary", "arbitrary")),
        name="swa",
    )(sinks, h, h, h, h, h)


def _rms_rows(x, g):
    return x * lax.rsqrt(jnp.mean(x * x, axis=-1, keepdims=True) + EPS) * g


def _mla_prep_kernel(cq_ref, ckv_ref, kr_ref, cos_ref, sin_ref, qg_ref, kvg_ref,
                     wq_ref, wk_ref, wv_ref, q_ref, k_ref, v_ref):
    scale = (MLA_NOPE_DIM + MLA_ROPE_DIM) ** -0.5 * math.log2(math.e)
    cosm, sinm = cos_ref[...], sin_ref[...]
    cqn = _rms_rows(cq_ref[...].astype(F32), qg_ref[...]).astype(BF16)
    ckvn = _rms_rows(ckv_ref[...].astype(F32), kvg_ref[...]).astype(BF16)
    q = jnp.dot(cqn, wq_ref[...], preferred_element_type=F32)
    kn = jnp.dot(ckvn, wk_ref[...], preferred_element_type=F32)
    v_ref[...] = jnp.dot(ckvn, wv_ref[...], preferred_element_type=F32).astype(v_ref.dtype)
    kr = _rope_block(kr_ref[...].astype(F32), cosm, sinm).astype(k_ref.dtype)
    for h in range(MLA_HEADS):
        b = h * MLA_QK_PAD
        q_ref[:, b:b + LANES] = (q[:, b:b + LANES] * scale).astype(q_ref.dtype)
        q_ref[:, b + LANES:b + 2 * LANES] = (
            _rope_block(q[:, b + LANES:b + 2 * LANES], cosm, sinm) * scale).astype(q_ref.dtype)
        k_ref[:, b:b + LANES] = kn[:, h * LANES:(h + 1) * LANES].astype(k_ref.dtype)
        k_ref[:, b + LANES:b + 2 * LANES] = kr


def _mla_prep(h, cosm, sinm, qg, kvg, wq, wk, wv, l):
    T = h.shape[0]
    tm = 512
    row = lambda c: (lambda i: (i, c))
    full = lambda a: _layer_spec(a, l)
    return pl.pallas_call(
        _mla_prep_kernel,
        grid=(T // tm,),
        in_specs=[pl.BlockSpec((tm, MLA_Q_RANK), row(H_CQ // MLA_Q_RANK)),
                  pl.BlockSpec((tm, MLA_KV_RANK), row(H_CKV // MLA_KV_RANK)),
                  pl.BlockSpec((tm, LANES), row(H_KR // LANES)),
                  pl.BlockSpec((tm, LANES), row(0)), pl.BlockSpec((tm, LANES), row(0)),
                  full(qg), full(kvg), full(wq), full(wk), full(wv)],
        out_specs=[pl.BlockSpec((tm, MLA_HEADS * MLA_QK_PAD), row(0)),
                   pl.BlockSpec((tm, MLA_HEADS * MLA_QK_PAD), row(0)),
                   pl.BlockSpec((tm, B_OUT), row(0))],
        out_shape=[jax.ShapeDtypeStruct((T, MLA_HEADS * MLA_QK_PAD), BF16),
                   jax.ShapeDtypeStruct((T, MLA_HEADS * MLA_QK_PAD), BF16),
                   jax.ShapeDtypeStruct((T, B_OUT), BF16)],
        compiler_params=_cparams(("arbitrary",)),
        name="mla_prep",
    )(h, h, h, cosm, sinm, qg, kvg, wq, wk, wv)


SCORE_CAP = 3.0e38
MLA_PAIR_UNROLL = 12


def _mla_attn_kernel(q_ref, k_ref, v_ref, o_ref, vp_ref, m_ref, acc_ref, bias_ref, sa_ref, sb_ref,
                     *, tq, nq):
    vp_ref[:, :MLA_V_DIM] = v_ref[...]
    vp_ref[:, MLA_V_DIM:] = jnp.ones((vp_ref.shape[0], MLA_V_DIM), vp_ref.dtype)

    @pl.when((pl.program_id(0) == 0) & (pl.program_id(1) == 0))
    def _():
        row = lax.broadcasted_iota(jnp.int32, (tq, tq), 0)
        col = lax.broadcasted_iota(jnp.int32, (tq, tq), 1)
        bias_ref[0] = jnp.full((tq, tq), SCORE_CAP, F32)
        bias_ref[1] = jnp.where(col <= row, SCORE_CAP, MASK_VALUE)
        m_ref[...] = jnp.full(m_ref.shape, MASK_VALUE, F32)
        acc_ref[...] = jnp.zeros(acc_ref.shape, F32)

    def rows(t):
        return pl.ds(pl.multiple_of(t * tq, tq), tq)

    def advance(i, j):
        last = j == i
        return jnp.where(last, i + 1, i), jnp.where(last, 0, j + 1)

    def scores(i, j, s_ref):
        i = jnp.minimum(i, nq - 1)
        s_ref[...] = lax.dot_general(q_ref[rows(i), :], k_ref[rows(j), :],
                                     (((1,), (1,)), ((), ())), preferred_element_type=F32)

    def consume(i, j, s_ref):
        s = jnp.minimum(s_ref[...], bias_ref[(j == i).astype(jnp.int32)])
        first = j == 0
        m_old = jnp.where(first, MASK_VALUE, m_ref[i])
        acc_old = jnp.where(first, 0.0, acc_ref[i])
        m_new = jnp.maximum(m_old, jnp.max(s, axis=-1, keepdims=True))
        p = jnp.exp2(s - jnp.tile(m_new, (1, tq // LANES))).astype(BF16)
        alpha = jnp.exp2(m_old - m_new)
        acc_ref[i] = jnp.tile(alpha, (1, 2 * MLA_V_DIM // LANES)) * acc_old + jnp.dot(
            p, vp_ref[rows(j), :], preferred_element_type=F32)
        m_ref[i] = m_new

    scores(0, 0, sa_ref)

    def unrolled_pairs(_, ij):
        bufs = (sa_ref, sb_ref)
        for u in range(MLA_PAIR_UNROLL):
            nxt = advance(*ij)
            scores(*nxt, bufs[(u + 1) % 2])
            consume(*ij, bufs[u % 2])
            ij = nxt
        return ij

    n_pairs = nq * (nq + 1) // 2
    assert n_pairs % MLA_PAIR_UNROLL == 0 and MLA_PAIR_UNROLL % 2 == 0
    lax.fori_loop(0, n_pairs // MLA_PAIR_UNROLL, unrolled_pairs, (jnp.int32(0), jnp.int32(0)))

    for i in range(nq):
        acc = acc_ref[i]
        o_ref[i * tq:(i + 1) * tq, :] = (acc[:, :MLA_V_DIM] / acc[:, MLA_V_DIM:]).astype(o_ref.dtype)


def _mla_attn(q, k, v, batch, seq):
    T = q.shape[0]
    tq = 512
    nq = seq // tq
    return pl.pallas_call(
        functools.partial(_mla_attn_kernel, tq=tq, nq=nq),
        grid=(batch, MLA_HEADS),
        in_specs=[pl.BlockSpec((seq, MLA_QK_PAD), lambda b, h: (b, h)),
                  pl.BlockSpec((seq, MLA_QK_PAD), lambda b, h: (b, h)),
                  pl.BlockSpec((seq, MLA_V_DIM), lambda b, h: (b, h))],
        out_specs=pl.BlockSpec((seq, MLA_V_DIM), lambda b, h: (b, h)),
        out_shape=jax.ShapeDtypeStruct((T, B_OUT), BF16),
        scratch_shapes=[pltpu.VMEM((seq, 2 * MLA_V_DIM), BF16),
                        pltpu.VMEM((nq, tq, LANES), F32),
                        pltpu.VMEM((nq, tq, 2 * MLA_V_DIM), F32),
                        pltpu.VMEM((2, tq, tq), F32),
                        pltpu.VMEM((tq, tq), F32), pltpu.VMEM((tq, tq), F32)],
        compiler_params=_cparams(("arbitrary", "arbitrary")),
        name="mla_attn",
    )(q, k, v)


def _gelu_erf(x):
    return 0.5 * x * (1.0 + lax.erf(x * (2.0 ** -0.5)))


def _sgu_kernel(hu_ref, hv_ref, g_ref, b_ref, w_ref, bt_ref, o_ref):
    u = _gelu_erf(hu_ref[...].astype(F32))
    v = _gelu_erf(hv_ref[...].astype(F32))
    vn = _layer_norm_rows(v, g_ref[...], b_ref[...]).astype(BF16)
    tm = vn.shape[0]
    r = lax.broadcasted_iota(jnp.int32, (SGU_CHUNK, SGU_CHUNK), 0)
    c = lax.broadcasted_iota(jnp.int32, (SGU_CHUNK, SGU_CHUNK), 1)
    causal = c <= r
    for g in range(SGU_GROUPS):
        w = jnp.where(causal, w_ref[g], 0.0).astype(BF16)
        bias = bt_ref[:, g:g + 1]
        cols = slice(g * SGU_GROUP_DIM, (g + 1) * SGU_GROUP_DIM)
        for n in range(tm // SGU_CHUNK):
            rows = slice(n * SGU_CHUNK, (n + 1) * SGU_CHUNK)
            mixed = jnp.dot(w, vn[rows, cols], preferred_element_type=F32) + bias
            o_ref[rows, cols] = (u[rows, cols] * mixed).astype(o_ref.dtype)


def _sgu(h, ln_g, ln_b, w_s, b_t, l):
    T = h.shape[0]
    tm = 512
    full2 = lambda a: _layer_spec(a, l)
    return pl.pallas_call(
        _sgu_kernel,
        grid=(T // tm,),
        in_specs=[pl.BlockSpec((tm, SGU_WIDTH), lambda i: (i, H_HU // SGU_WIDTH)),
                  pl.BlockSpec((tm, SGU_WIDTH), lambda i: (i, H_HV // SGU_WIDTH)),
                  full2(ln_g), full2(ln_b), full2(w_s), full2(b_t)],
        out_specs=pl.BlockSpec((tm, SGU_WIDTH), lambda i: (i, 0)),
        out_shape=jax.ShapeDtypeStruct((T, SGU_WIDTH), BF16),
        compiler_params=_cparams(("arbitrary",)),
        name="sgu",
    )(h, h, ln_g, ln_b, w_s, b_t)


def _merge_kernel(x_ref, ya_ref, yb_ref, yc_ref, gl_ref, bg_ref, pa_ref, pb_ref, pc_ref,
                  wo_ref, g_ref, b_ref, o_ref):
    merged = None
    for br, (y_ref, p_ref) in enumerate(((ya_ref, pa_ref), (yb_ref, pb_ref), (yc_ref, pc_ref))):
        cols = slice(br * D_MODEL, (br + 1) * D_MODEL)
        gate = jax.nn.sigmoid(gl_ref[:, cols].astype(F32) + bg_ref[br:br + 1, :])
        term = gate * jnp.dot(y_ref[...], p_ref[...], preferred_element_type=F32)
        merged = term if merged is None else merged + term
    y = DN_ALPHA * x_ref[...] + jnp.dot(merged.astype(BF16), wo_ref[...],
                                        preferred_element_type=F32)
    o_ref[...] = _layer_norm_rows(y, g_ref[...], b_ref[...])


def _merge(x2d, ya, yb, yc, h, b_gate, pa, pb, pc, wo, ln_g, ln_b, l):
    T = x2d.shape[0]
    tm = 256
    row = lambda a_w: pl.BlockSpec((tm, a_w), lambda i: (i, 0))
    res = lambda a: pl.BlockSpec((None,) + a.shape[1:], lambda i: (l,) + (0,) * (a.ndim - 1),
                                 pipeline_mode=pl.Buffered(1))
    return pl.pallas_call(
        _merge_kernel,
        grid=(T // tm,),
        in_specs=[row(D_MODEL), row(A_Q), row(B_OUT), row(SGU_WIDTH),
                  row(N_BRANCHES * D_MODEL), res(b_gate), res(pa), res(pb), res(pc), res(wo),
                  res(ln_g), res(ln_b)],
        out_specs=row(D_MODEL),
        out_shape=jax.ShapeDtypeStruct((T, D_MODEL), F32),
        compiler_params=_cparams(("arbitrary",)),
        name="merge",
    )(x2d, ya, yb, yc, h, b_gate, pa, pb, pc, wo, ln_g, ln_b)


FFN_HALO = 8


def _causal_conv(u, prev, cw, cb):
    tm = u.shape[0]
    ext = jnp.concatenate([prev, u], axis=0)
    u1 = pltpu.roll(ext, 1, axis=0)[FFN_HALO:FFN_HALO + tm]
    u2 = pltpu.roll(ext, 2, axis=0)[FFN_HALO:FFN_HALO + tm]
    return cb + cw[0:1] * u2 + cw[1:2] * u1 + cw[2:3] * u


def _ffn_kernel(x_ref, wg_ref, wv_ref, cwg_ref, cwv_ref, cbg_ref, cbv_ref, wd_ref, g_ref, b_ref,
                o_ref, xb_ref, acc_ref, halo_ref, *, tiles_per_seq):
    i, j = pl.program_id(0), pl.program_id(1)
    tm = x_ref.shape[0]

    @pl.when(j == 0)
    def _():
        xb_ref[...] = x_ref[...].astype(BF16)
        acc_ref[...] = jnp.zeros_like(acc_ref)

    @pl.when(i % tiles_per_seq == 0)
    def _():
        halo_ref[j] = jnp.zeros(halo_ref.shape[1:], F32)

    xb = xb_ref[...]
    ug = jnp.dot(xb, wg_ref[...], preferred_element_type=F32)
    uv = jnp.dot(xb, wv_ref[...], preferred_element_type=F32)
    fc = ug.shape[1]
    halo = halo_ref[j]
    gate = _causal_conv(ug, halo[:, :fc], cwg_ref[...], cbg_ref[...])
    val = _causal_conv(uv, halo[:, fc:], cwv_ref[...], cbv_ref[...])
    halo_ref[j] = jnp.concatenate([ug[tm - FFN_HALO:], uv[tm - FFN_HALO:]], axis=1)
    act = (gate * jax.nn.sigmoid(gate) * val).astype(BF16)
    acc_ref[...] += jnp.dot(act, wd_ref[...], preferred_element_type=F32)

    @pl.when(j == pl.num_programs(1) - 1)
    def _():
        y = DN_ALPHA * x_ref[...] + acc_ref[...]
        o_ref[...] = _layer_norm_rows(y, g_ref[...], b_ref[...])


def _ffn(x2d, w_up, conv_w, conv_b, w_down, ln_g, ln_b, l, seq):
    T = x2d.shape[0]
    tm, fc = 512, 512
    nf = D_FF // fc
    gate_c = lambda i, j: (l, 0, j)
    val_c = lambda i, j: (l, 0, nf + j)
    return pl.pallas_call(
        functools.partial(_ffn_kernel, tiles_per_seq=seq // tm),
        grid=(T // tm, nf),
        in_specs=[pl.BlockSpec((tm, D_MODEL), lambda i, j: (i, 0)),
                  pl.BlockSpec((None, D_MODEL, fc), gate_c),
                  pl.BlockSpec((None, D_MODEL, fc), val_c),
                  pl.BlockSpec((None, CONV_WIDTH, fc), gate_c),
                  pl.BlockSpec((None, CONV_WIDTH, fc), val_c),
                  pl.BlockSpec((None, 1, fc), gate_c), pl.BlockSpec((None, 1, fc), val_c),
                  pl.BlockSpec((None, fc, D_MODEL), lambda i, j: (l, j, 0)),
                  _layer_spec(ln_g, l), _layer_spec(ln_b, l)],
        out_specs=pl.BlockSpec((tm, D_MODEL), lambda i, j: (i, 0)),
        out_shape=jax.ShapeDtypeStruct((T, D_MODEL), F32),
        scratch_shapes=[pltpu.VMEM((tm, D_MODEL), BF16), pltpu.VMEM((tm, D_MODEL), F32),
                        pltpu.VMEM((nf, FFN_HALO, 2 * fc), F32)],
        compiler_params=_cparams(("arbitrary", "arbitrary")),
        name="ffn",
    )(x2d, w_up, w_up, conv_w, conv_w, conv_b, conv_b, w_down, ln_g, ln_b)


def _reorder_w_in(w):
    o = 0
    qa = w[..., o:o + A_Q]; o += A_Q
    ka = w[..., o:o + A_KV]; o += A_KV
    va = w[..., o:o + A_KV]; o += A_KV
    cq = w[..., o:o + MLA_Q_RANK]; o += MLA_Q_RANK
    ckv = w[..., o:o + MLA_KV_RANK]; o += MLA_KV_RANK
    kr = w[..., o:o + MLA_ROPE_DIM]; o += MLA_ROPE_DIM
    hu = w[..., o:o + SGU_WIDTH]; o += SGU_WIDTH
    hv = w[..., o:o + SGU_WIDTH]; o += SGU_WIDTH
    g = w[..., o:]
    pad = jnp.zeros(w.shape[:-1] + (H_WIDTH - H_KR - MLA_ROPE_DIM,), w.dtype)
    return jnp.concatenate([g, qa, hu, hv, cq, ckv, ka, va, kr, pad], axis=-1).astype(BF16)


def _pad_w_uq(w):
    lead = w.shape[:-1]
    w = w.reshape(lead + (MLA_HEADS, MLA_NOPE_DIM + MLA_ROPE_DIM))
    w = jnp.pad(w, ((0, 0),) * (w.ndim - 1) + ((0, MLA_QK_PAD - MLA_NOPE_DIM - MLA_ROPE_DIM),))
    return w.reshape(lead + (MLA_HEADS * MLA_QK_PAD,)).astype(BF16)


def _split_w_ukv(w):
    lead = w.shape[:-1]
    w = w.reshape(lead + (MLA_HEADS, MLA_NOPE_DIM + MLA_V_DIM))
    wk = w[..., :MLA_NOPE_DIM].reshape(lead + (MLA_HEADS * MLA_NOPE_DIM,))
    wv = w[..., MLA_NOPE_DIM:].reshape(lead + (MLA_HEADS * MLA_V_DIM,))
    return wk.astype(BF16), wv.astype(BF16)


def kernel(x, positions, w_in, b_gate, sinks, q_norm_g, kv_norm_g, w_uq, w_ukv, sgu_ln_g, sgu_ln_b, sgu_w, sgu_b, w_proj_a, w_proj_b, w_proj_c, w_o, ln1_g, ln1_b, w_up, conv_w, conv_b, w_down, ln2_g, ln2_b):
    B, S, D = x.shape
    T = B * S
    x2d = x.reshape(T, D)
    cosm, sinm = _rope_tables(positions.reshape(T, 1))
    row = lambda a: a.reshape(a.shape[0], 1, -1)
    w_in_b = _reorder_w_in(w_in)
    w_uq_b = _pad_w_uq(w_uq)
    w_uk_b, w_uv_b = _split_w_ukv(w_ukv)
    pa, pb, pc, wo = (w.astype(BF16) for w in (w_proj_a, w_proj_b, w_proj_c, w_o))
    w_up_b, w_down_b = w_up.astype(BF16), w_down.astype(BF16)
    sgu_bt = jnp.swapaxes(sgu_b, 1, 2)
    for l in range(DEPTH):
        h = _inproj(x2d, w_in_b, l)
        y_a = _swa(h, sinks, l, B, S)
        q, k, v = _mla_prep(h, cosm, sinm, row(q_norm_g), row(kv_norm_g), w_uq_b, w_uk_b, w_uv_b, l)
        y_b = _mla_attn(q, k, v, B, S)
        y_c = _sgu(h, row(sgu_ln_g), row(sgu_ln_b), sgu_w, sgu_bt, l)
        x2d = _merge(x2d, y_a, y_b, y_c, h, b_gate, pa, pb, pc, wo, row(ln1_g), row(ln1_b), l)
        x2d = _ffn(x2d, w_up_b, conv_w, row(conv_b), w_down_b, row(ln2_g), row(ln2_b), l, S)
    return x2d.reshape(B, S, D)
```
